```python
import jax, jax.numpy as jnp
from jax import lax
import numpy as np

D_MODEL = 1024
BATCH = 32
SEQ = 2048
DEPTH = 4

HG_HEADS = 8
HG_DK = D_MODEL // HG_HEADS
HG_DV = D_MODEL // HG_HEADS
HG_WIDTH = HG_HEADS * HG_DK
HG_CHUNK = 32
S5_WIDTH = D_MODEL
S5_GROUP = 16
S5_GROUPS = S5_WIDTH // S5_GROUP
S5_STATE = 64
S5_DT_MIN = 1e-3
S5_DT_MAX = 1e-1
PEER_HEADS = 8
PEER_NKEYS = 128
PEER_EXPERTS = PEER_NKEYS * PEER_NKEYS
PEER_QDIM = 256
PEER_HALF = PEER_QDIM // 2
PEER_TOPK = 16
PEER_BLOCK = 128
IN_COLS = 4 * HG_WIDTH + S5_WIDTH + 2 * D_MODEL
N_MOD = 6
EPS = 1e-6

kernel_name = "hybrid_hgrn2_s5_peer_adaln"


def rms_norm(x, g):
    x32 = x.astype(jnp.float32)
    y = x32 * lax.rsqrt(jnp.mean(x32 * x32, axis=-1, keepdims=True) + EPS)
    return (y * g.astype(jnp.float32)).astype(x.dtype)


def modulate(h, shift, scale):
    return h * (1.0 + scale[:, None, :]) + shift[:, None, :]


def hgrn2_branch(q_pre, f_pre, i_in, og, lb, norm_g):
    dt = q_pre.dtype
    B_, L = q_pre.shape[0], q_pre.shape[1]
    n_chunks = L // HG_CHUNK
    f32 = f_pre.astype(jnp.float32)
    q = jax.nn.silu(q_pre.astype(jnp.float32))
    log_f = jnp.logaddexp(jnp.log(lb), jnp.log1p(-lb) + jax.nn.log_sigmoid(f32))
    k = (1.0 - lb) * jax.nn.sigmoid(-f32)
    v = i_in.astype(jnp.float32)

    def to_chunks(t, d):
        return t.reshape(B_, n_chunks, HG_CHUNK, HG_HEADS, d).transpose(1, 0, 3, 2, 4)

    qs, ks, gs = to_chunks(q, HG_DK), to_chunks(k, HG_DK), to_chunks(log_f, HG_DK)
    vs = to_chunks(v, HG_DV)
    causal = jnp.tril(jnp.ones((HG_CHUNK, HG_CHUNK), dtype=bool))[:, :, None]

    def step(S, inp):
        qc, kc, vc, gc = inp
        b = jnp.cumsum(gc, axis=-2)
        inter = jnp.einsum('bhck,bhkv->bhcv', qc * jnp.exp(b), S)
        diff = b[:, :, :, None, :] - b[:, :, None, :, :]
        decay = jnp.where(causal, jnp.exp(jnp.where(causal, diff, 0.0)), 0.0)
        att = jnp.einsum('bhtk,bhsk,bhtsk->bhts', qc, kc, decay)
        intra = jnp.einsum('bhts,bhsv->bhtv', att, vc)
        b_last = b[:, :, -1:, :]
        S_new = jnp.exp(b_last[:, :, 0, :])[..., None] * S + jnp.einsum(
            'bhsk,bhsv->bhkv', kc * jnp.exp(b_last - b), vc)
        return S_new, inter + intra

    S0 = jnp.zeros((B_, HG_HEADS, HG_DK, HG_DV), jnp.float32)
    _, o = lax.scan(step, S0, (qs, ks, vs, gs))
    o = o.transpose(1, 0, 3, 2, 4).reshape(B_, L, HG_HEADS, HG_DV)
    o = o * lax.rsqrt(jnp.mean(o * o, axis=-1, keepdims=True) + EPS) * norm_g.astype(jnp.float32)
    o = o.reshape(B_, L, HG_HEADS * HG_DV) * jax.nn.silu(og.astype(jnp.float32))
    return o.astype(dt)


def s5_branch(u, a_re, a_im, log_dt, b_re, b_im, c_re, c_im, d_skip):
    dt_in = u.dtype
    B_, L = u.shape[0], u.shape[1]
    ug = u.astype(jnp.float32).reshape(B_, L, S5_GROUPS, S5_GROUP)
    a_re, a_im = a_re.astype(jnp.float32), a_im.astype(jnp.float32)
    b_re, b_im = b_re.astype(jnp.float32), b_im.astype(jnp.float32)
    c_re, c_im = c_re.astype(jnp.float32), c_im.astype(jnp.float32)
    step = jnp.exp(log_dt.astype(jnp.float32))[:, None]
    mag = jnp.exp(step * a_re)
    ang = step * a_im
    ab_re, ab_im = mag * jnp.cos(ang), mag * jnp.sin(ang)
    den = a_re * a_re + a_im * a_im
    z_re = ((ab_re - 1.0) * a_re + ab_im * a_im) / den
    z_im = (ab_im * a_re - (ab_re - 1.0) * a_im) / den
    bb_re = z_re[..., None] * b_re - z_im[..., None] * b_im
    bb_im = z_re[..., None] * b_im + z_im[..., None] * b_re
    bu_re = jnp.einsum('blgc,gpc->blgp', ug, bb_re)
    bu_im = jnp.einsum('blgc,gpc->blgp', ug, bb_im)
    ar = jnp.broadcast_to(ab_re, bu_re.shape)
    ai = jnp.broadcast_to(ab_im, bu_im.shape)

    def combine(e1, e2):
        a1r, a1i, b1r, b1i = e1
        a2r, a2i, b2r, b2i = e2
        return (a1r * a2r - a1i * a2i,
                a1r * a2i + a1i * a2r,
                a2r * b1r - a2i * b1i + b2r,
                a2r * b1i + a2i * b1r + b2i)

    _, _, xr, xi = lax.associative_scan(combine, (ar, ai, bu_re, bu_im), axis=1)
    y = (jnp.einsum('blgp,gcp->blgc', xr, c_re) - jnp.einsum('blgp,gcp->blgc', xi, c_im)
         + d_skip.astype(jnp.float32).reshape(S5_GROUPS, S5_GROUP) * ug)
    return y.reshape(B_, L, S5_WIDTH).astype(dt_in)


def peer_layer(h, wq, subkeys, u_tab, v_tab):
    B_, L, D = h.shape
    dt = h.dtype
    n_blocks = (B_ * L) // PEER_BLOCK
    blocks = h.reshape(n_blocks, PEER_BLOCK, D)
    keys32 = subkeys.astype(jnp.float32)

    def peer_block(hb):
        q = (hb @ wq).astype(jnp.float32).reshape(PEER_BLOCK, PEER_HEADS, 2, PEER_HALF)
        s = jnp.einsum('thpd,hpnd->thpn', q, keys32)
        s1_v, s1_i = lax.top_k(s[:, :, 0, :], PEER_TOPK)
        s2_v, s2_i = lax.top_k(s[:, :, 1, :], PEER_TOPK)
        cand_v = (s1_v[..., :, None] + s2_v[..., None, :]).reshape(PEER_BLOCK, PEER_HEADS, PEER_TOPK * PEER_TOPK)
        cand_i = (s1_i[..., :, None] * PEER_NKEYS + s2_i[..., None, :]).reshape(PEER_BLOCK, PEER_HEADS, PEER_TOPK * PEER_TOPK)
        top_v, top_j = lax.top_k(cand_v, PEER_TOPK)
        idx = jnp.take_along_axis(cand_i, top_j, axis=-1)
        gate = jax.nn.softmax(top_v, axis=-1)
        u_sel = jnp.take(u_tab, idx, axis=0)
        v_sel = jnp.take(v_tab, idx, axis=0)
        act = jax.nn.gelu(jnp.einsum('thkd,td->thk', u_sel, hb).astype(jnp.float32))
        return jnp.einsum('thk,thkd->td', (gate * act).astype(dt), v_sel)

    out = lax.map(peer_block, blocks)
    return out.reshape(B_, L, D)


def setup_inputs(seed: int = 0) -> dict:
    key = jax.random.key(seed)
    ks = jax.random.split(key, 26)
    f = jnp.float32
    nrm = lambda k, shape, s: jax.random.normal(k, shape, f) * s
    n_idx = jnp.arange(S5_STATE, dtype=f)
    inp = {}
    inp['x'] = nrm(ks[0], (BATCH, SEQ, D_MODEL), 1.0)
    inp['c'] = nrm(ks[1], (BATCH, D_MODEL), 1.0)
    inp['ada_w'] = nrm(ks[2], (DEPTH, D_MODEL, N_MOD * D_MODEL), 0.5 * D_MODEL ** -0.5)
    inp['ada_b'] = nrm(ks[3], (DEPTH, N_MOD * D_MODEL), 0.02)
    inp['norm1_g'] = 1.0 + nrm(ks[4], (DEPTH, D_MODEL), 0.02)
    inp['norm2_g'] = 1.0 + nrm(ks[5], (DEPTH, D_MODEL), 0.02)
    inp['w_in'] = nrm(ks[6], (DEPTH, D_MODEL, IN_COLS), D_MODEL ** -0.5)
    inp['hg_lb_logits'] = nrm(ks[7], (DEPTH, HG_WIDTH), 0.5)
    inp['hg_norm_g'] = 1.0 + nrm(ks[8], (DEPTH, HG_DV), 0.02)
    inp['s5_a_re'] = -0.5 * (1.0 + nrm(ks[9], (DEPTH, S5_GROUPS, S5_STATE), 0.05))
    inp['s5_a_im'] = jnp.pi * n_idx + nrm(ks[10], (DEPTH, S5_GROUPS, S5_STATE), 0.01)
    inp['s5_log_dt'] = jax.random.uniform(ks[11], (DEPTH, S5_GROUPS), f,
                                          np.log(S5_DT_MIN).astype(np.float32), np.log(S5_DT_MAX).astype(np.float32))
    inp['s5_b_re'] = nrm(ks[12], (DEPTH, S5_GROUPS, S5_STATE, S5_GROUP), (2.0 * S5_GROUP) ** -0.5)
    inp['s5_b_im'] = nrm(ks[13], (DEPTH, S5_GROUPS, S5_STATE, S5_GROUP), (2.0 * S5_GROUP) ** -0.5)
    inp['s5_c_re'] = nrm(ks[14], (DEPTH, S5_GROUPS, S5_GROUP, S5_STATE), (2.0 * S5_STATE) ** -0.5)
    inp['s5_c_im'] = nrm(ks[15], (DEPTH, S5_GROUPS, S5_GROUP, S5_STATE), (2.0 * S5_STATE) ** -0.5)
    inp['s5_d'] = nrm(ks[16], (DEPTH, S5_WIDTH), 1.0)
    inp['glu_w'] = nrm(ks[17], (DEPTH, S5_WIDTH, S5_WIDTH), S5_WIDTH ** -0.5)
    inp['glu_b'] = nrm(ks[18], (DEPTH, S5_WIDTH), 0.02)
    inp['w_out'] = nrm(ks[19], (DEPTH, D_MODEL, D_MODEL), D_MODEL ** -0.5)
    inp['peer_wq'] = nrm(ks[20], (DEPTH, D_MODEL, PEER_HEADS * PEER_QDIM), D_MODEL ** -0.5)
    inp['peer_subkeys'] = nrm(ks[21], (DEPTH, PEER_HEADS, 2, PEER_NKEYS, PEER_HALF), PEER_HALF ** -0.5)
    inp['peer_u'] = nrm(ks[22], (DEPTH, PEER_EXPERTS, D_MODEL), D_MODEL ** -0.5)
    inp['peer_v'] = nrm(ks[23], (DEPTH, PEER_EXPERTS, D_MODEL), PEER_HEADS ** -0.5)
    inp['final_g'] = 1.0 + nrm(ks[24], (D_MODEL,), 0.02)
    return inp


def reference(x, c, ada_w, ada_b, norm1_g, norm2_g, w_in, hg_lb_logits, hg_norm_g,
              s5_a_re, s5_a_im, s5_log_dt, s5_b_re, s5_b_im, s5_c_re, s5_c_im, s5_d,
              glu_w, glu_b, w_out, peer_wq, peer_subkeys, peer_u, peer_v, final_g):
    lb_cum = jnp.cumsum(jax.nn.softmax(hg_lb_logits.astype(jnp.float32), axis=0), axis=0)
    lb_all = lb_cum - lb_cum[0:1]
    c_act = jax.nn.silu(c)
    splits = [HG_WIDTH, 2 * HG_WIDTH, 3 * HG_WIDTH, 4 * HG_WIDTH,
              4 * HG_WIDTH + S5_WIDTH, 4 * HG_WIDTH + S5_WIDTH + D_MODEL]
    for l in range(DEPTH):
        mod = c_act @ ada_w[l] + ada_b[l]
        sh1, sc1, g1, sh2, sc2, g2 = jnp.split(mod, N_MOD, axis=-1)
        h = modulate(rms_norm(x, norm1_g[l]), sh1, sc1)
        proj = h @ w_in[l]
        q_pre, f_pre, i_in, og, u_s5, ga, gb = jnp.split(proj, splits, axis=-1)
        y_a = hgrn2_branch(q_pre, f_pre, i_in, og, lb_all[l], hg_norm_g[l])
        y_s = jax.nn.gelu(s5_branch(u_s5, s5_a_re[l], s5_a_im[l], s5_log_dt[l], s5_b_re[l], s5_b_im[l],
                                    s5_c_re[l], s5_c_im[l], s5_d[l]))
        y_b = y_s * jax.nn.sigmoid(y_s @ glu_w[l] + glu_b[l])
        merged = jax.nn.sigmoid(ga) * y_a + jax.nn.sigmoid(gb) * y_b
        x = x + g1[:, None, :] * (merged @ w_out[l])
        h2 = modulate(rms_norm(x, norm2_g[l]), sh2, sc2)
        x = x + g2[:, None, :] * peer_layer(h2, peer_wq[l], peer_subkeys[l], peer_u[l], peer_v[l])
    return rms_norm(x, final_g)
```

```python
import functools

import jax
import jax.numpy as jnp
from jax import lax
from jax.experimental import pallas as pl
from jax.experimental.pallas import tpu as pltpu

F32 = jnp.float32
BF16 = jnp.bfloat16
EPS = 1e-6

V7X_VMEM_BYTES = 64 * 1024 * 1024
V7X_LANES = 128
VMEM_LIMIT = V7X_VMEM_BYTES - 8 * 1024 * 1024

HG_HEADS = 8
HG_CHUNK = 32
S5_GROUP = 16
S5_STATE = 64
PEER_HEADS = 8
PEER_NKEYS = 128
PEER_TOPK = 16
N_MOD = 6
NOT_SELECTED = 99.0
NEG_INF = float("-inf")


def _cparams(sem):
    return pltpu.CompilerParams(dimension_semantics=sem, vmem_limit_bytes=VMEM_LIMIT)


def _dot(a, b):
    return jnp.dot(a, b, preferred_element_type=F32)


def _dot_nt(a, b):
    return lax.dot_general(a, b, (((1,), (1,)), ((), ())), preferred_element_type=F32)


def _dot_tn(a, b):
    return lax.dot_general(a, b, (((0,), (0,)), ((), ())), preferred_element_type=F32)


def _sigmoid(x):
    return 1.0 / (1.0 + jnp.exp(-x))


def _silu(x):
    return x * _sigmoid(x)


def _gelu_tanh(x):
    c = 0.7978845608028654
    return 0.5 * x * (1.0 + jnp.tanh(c * (x + 0.044715 * (x * x * x))))


def _norm_mod(x, g, sc, sh):
    y = x * lax.rsqrt(jnp.mean(x * x, axis=-1, keepdims=True) + EPS) * g
    return y * (1.0 + sc) + sh


def _mod_kernel(c_ref, w_ref, b_ref, o_ref):
    ca = _silu(c_ref[...]).astype(BF16)
    o_ref[...] = _dot(ca, w_ref[...].astype(BF16)) + b_ref[...]


def _adaln_mod(c, ada_w, ada_b):
    depth, d, nd = ada_w.shape
    bsz = c.shape[0]
    tn = d
    return pl.pallas_call(
        _mod_kernel,
        grid=(depth, nd // tn),
        in_specs=[
            pl.BlockSpec((bsz, d), lambda l, j: (0, 0)),
            pl.BlockSpec((None, d, tn), lambda l, j: (l, 0, j)),
            pl.BlockSpec((None, 1, tn), lambda l, j: (l, 0, j)),
        ],
        out_specs=pl.BlockSpec((None, bsz, tn), lambda l, j: (l, 0, j)),
        out_shape=jax.ShapeDtypeStruct((depth, bsz, nd), F32),
        compiler_params=_cparams(("arbitrary", "arbitrary")),
        name="adaln_mod",
    )(c, ada_w, ada_b.reshape(depth, 1, nd))


def _lb_kernel(z_ref, o_ref):
    z = z_ref[...]
    depth = z.shape[0]
    m = jnp.max(z, axis=0, keepdims=True)
    e = jnp.exp(z - m)
    p = e / jnp.sum(e, axis=0, keepdims=True)
    run = jnp.zeros_like(m)
    for l in range(depth):
        if l > 0:
            run = run + p[l:l + 1]
        o_ref[l] = run


def _lower_bounds(logits):
    depth, w = logits.shape
    return pl.pallas_call(
        _lb_kernel,
        out_shape=jax.ShapeDtypeStruct((depth, 1, w), F32),
        name="hgrn_lower_bounds",
    )(logits)


def _s5_prep_kernel(are_ref, aim_ref, ldt_ref, bre_ref, bim_ref,
                    abre_ref, abim_ref, bbre_ref, bbim_ref):
    a_re = are_ref[...]
    a_im = aim_ref[...]
    step = jnp.exp(ldt_ref[...])
    mag = jnp.exp(step * a_re)
    ang = step * a_im
    ab_re = mag * jnp.cos(ang)
    ab_im = mag * jnp.sin(ang)
    den = a_re * a_re + a_im * a_im
    z_re = ((ab_re - 1.0) * a_re + ab_im * a_im) / den
    z_im = (ab_im * a_re - (ab_re - 1.0) * a_im) / den
    b_re = bre_ref[...]
    b_im = bim_ref[...]
    abre_ref[...] = ab_re
    abim_ref[...] = ab_im
    bbre_ref[...] = z_re * b_re - z_im * b_im
    bbim_ref[...] = z_re * b_im + z_im * b_re


def _s5_discretise(a_re, a_im, log_dt, b_re_t, b_im_t):
    g, p = a_re.shape
    c = b_re_t.shape[1]
    return pl.pallas_call(
        _s5_prep_kernel,
        out_shape=(jax.ShapeDtypeStruct((g, 1, p), F32), jax.ShapeDtypeStruct((g, 1, p), F32),
                   jax.ShapeDtypeStruct((g, c, p), F32), jax.ShapeDtypeStruct((g, c, p), F32)),
        name="s5_discretise",
    )(a_re.reshape(g, 1, p), a_im.reshape(g, 1, p),
      jnp.broadcast_to(log_dt.reshape(g, 1, 1), (g, 1, p)), b_re_t, b_im_t)


def _proj_kernel(x_ref, g_ref, sc_ref, sh_ref, w_ref, o_ref, h_ref):
    @pl.when(pl.program_id(2) == 0)
    def _():
        h_ref[...] = _norm_mod(x_ref[...], g_ref[...], sc_ref[...], sh_ref[...]).astype(BF16)

    o_ref[...] = _dot(h_ref[...], w_ref[...]).astype(o_ref.dtype)


def _norm_proj(x, g, mod, sc_idx, sh_idx, w, *, tm, tn, time_major):
    bsz, seq, d = x.shape
    nc = w.shape[1]
    ncb = nc // tn
    if time_major:
        out_shape = jax.ShapeDtypeStruct((seq, bsz * nc), F32)
        out_spec = pl.BlockSpec((tm, tn), lambda b, i, n: (i, b * ncb + n))
    else:
        out_shape = jax.ShapeDtypeStruct((bsz, seq, nc), F32)
        out_spec = pl.BlockSpec((None, tm, tn), lambda b, i, n: (b, i, n))
    return pl.pallas_call(
        _proj_kernel,
        grid=(bsz, seq // tm, ncb),
        in_specs=[
            pl.BlockSpec((None, tm, d), lambda b, i, n: (b, i, 0)),
            pl.BlockSpec((1, d), lambda b, i, n: (0, 0)),
            pl.BlockSpec((None, None, 1, d), lambda b, i, n: (b, sc_idx, 0, 0)),
            pl.BlockSpec((None, None, 1, d), lambda b, i, n: (b, sh_idx, 0, 0)),
            pl.BlockSpec((d, tn), lambda b, i, n: (0, n)),
        ],
        out_specs=out_spec,
        out_shape=out_shape,
        scratch_shapes=[pltpu.VMEM((tm, d), BF16)],
        compiler_params=_cparams(("arbitrary", "arbitrary", "arbitrary")),
        name="norm_proj_tm" if time_major else "norm_proj",
    )(x, g, mod, mod, w)


def _split3(x):
    hi = x.astype(BF16)
    r1 = x - hi.astype(F32)
    mid = r1.astype(BF16)
    lo = (r1 - mid.astype(F32)).astype(BF16)
    return hi, mid, lo


def _hgrn_kernel(q_ref, f_ref, i_ref, og_ref, lb_ref, ng_ref, o_ref, st_ref, *, heads):
    tt, width = q_ref.shape
    dk = width // heads
    c = HG_CHUNK
    levels = c.bit_length() - 1

    @pl.when(pl.program_id(1) == 0)
    def _():
        st_ref[...] = jnp.zeros_like(st_ref)

    lb = lb_ref[...]
    log_lb = jnp.log(lb)
    log_1m_lb = jnp.log1p(-lb)
    one_m_lb = 1.0 - lb
    ng = ng_ref[...]

    row = lax.broadcasted_iota(jnp.int32, (c, width), 0)
    t_i = lax.broadcasted_iota(jnp.int32, (c, c), 0)
    s_i = lax.broadcasted_iota(jnp.int32, (c, c), 1)
    tri = (s_i <= t_i).astype(BF16)

    def chunk_body(ci, carry):
        r0 = pl.multiple_of(ci * c, c)
        rows = pl.ds(r0, c)
        fpre = f_ref[rows, :]
        log_sig = jnp.minimum(fpre, 0.0) - jnp.log1p(jnp.exp(-jnp.abs(fpre)))
        u = log_1m_lb + log_sig
        mx = jnp.maximum(log_lb, u)
        g = mx + jnp.log1p(jnp.exp(-jnp.abs(log_lb - u)))
        kk = one_m_lb * _sigmoid(-fpre)
        qq = _silu(q_ref[rows, :])
        vv = i_ref[rows, :].astype(BF16)

        g_hi, g_mid, g_lo = _split3(g)
        b = _dot(tri, g_hi) + _dot(tri, g_mid) + _dot(tri, g_lo)
        b_last = b[c - 1:c, :]
        q_inter = (qq * jnp.exp(b)).astype(BF16)
        k_state = (kk * jnp.exp(b_last - b)).astype(BF16)
        decay_last = jnp.exp(b_last)

        q_lv, k_lv = [], []
        f_p = b
        for p in range(levels):
            if p > 0:
                half = 1 << (p - 1)
                up = pltpu.roll(f_p, c - half, 0)
                f_p = jnp.where((row & half) == 0, up, f_p)
            step = 1 << p
            prev = pltpu.roll(f_p, step, 0)
            hi_half = (row & step) != 0
            eq = jnp.exp(jnp.where(hi_half, b - prev, NEG_INF))
            ek = jnp.exp(jnp.where(hi_half, NEG_INF, f_p - b))
            q_lv.append((qq * eq).astype(BF16))
            k_lv.append((kk * ek).astype(BF16))
        qk = qq * kk

        for h in range(heads):
            cols = slice(h * dk, (h + 1) * dk)
            att = jnp.where(t_i == s_i, jnp.sum(qk[:, cols], axis=-1, keepdims=True), 0.0)
            for p in range(levels):
                a_p = _dot_nt(q_lv[p][:, cols], k_lv[p][:, cols])
                same_parent = (t_i >> (p + 1)) == (s_i >> (p + 1))
                att = att + jnp.where(same_parent, a_p, 0.0)
            st = st_ref[h]
            inter = _dot_nt(q_inter[:, cols], st.astype(BF16))
            intra = _dot(att.astype(BF16), vv[:, cols])
            o = inter + intra
            st_ref[h] = st * decay_last[:, cols] + _dot_tn(vv[:, cols], k_state[:, cols])
            o = o * lax.rsqrt(jnp.mean(o * o, axis=-1, keepdims=True) + EPS) * ng
            o_ref[rows, cols] = o * _silu(og_ref[rows, cols])
        return carry

    lax.fori_loop(0, tt // c, chunk_body, 0)


def _hgrn2(proj, lb, norm_g, *, tt, heads):
    bsz, seq, _ = proj.shape
    width = lb.shape[-1]
    dv = norm_g.shape[-1]
    col = lambda j: pl.BlockSpec((None, tt, width), lambda b, i, j=j: (b, i, j))
    return pl.pallas_call(
        functools.partial(_hgrn_kernel, heads=heads),
        grid=(bsz, seq // tt),
        in_specs=[col(0), col(1), col(2), col(3),
                  pl.BlockSpec((1, width), lambda b, i: (0, 0)),
                  pl.BlockSpec((1, dv), lambda b, i: (0, 0))],
        out_specs=pl.BlockSpec((None, tt, width), lambda b, i: (b, i, 0)),
        out_shape=jax.ShapeDtypeStruct((bsz, seq, width), F32),
        scratch_shapes=[pltpu.VMEM((heads, dv, width // heads), F32)],
        compiler_params=_cparams(("arbitrary", "arbitrary")),
        name="hgrn2",
    )(proj, proj, proj, proj, lb, norm_g)


def _s5_kernel(u_ref, a_ref, bm_ref, cm_ref, d_ref, gw_ref, gb_ref, o_ref,
               st_ref, bu_ref, x_ref, ys_ref, *, bsz):
    rows, width = u_ref.shape
    nblk, _, two_p = bm_ref.shape
    pb = two_p // 2
    tt = rows // bsz

    @pl.when(pl.program_id(0) == 0)
    def _():
        st_ref[...] = jnp.zeros_like(st_ref)

    for j in range(nblk):
        cols = slice(j * V7X_LANES, (j + 1) * V7X_LANES)
        uj = u_ref[:, cols]
        bu_ref[...] = _dot(uj.astype(BF16), bm_ref[j])
        ar = a_ref[j, 0:1, :]
        ai = a_ref[j, 1:2, :]

        def step(t, carry):
            xr, xi = carry
            r0 = pl.multiple_of(t * bsz, bsz)
            bur = bu_ref[pl.ds(r0, bsz), 0:pb]
            bui = bu_ref[pl.ds(r0, bsz), pb:two_p]
            nr = ar * xr - ai * xi + bur
            ni = ar * xi + ai * xr + bui
            x_ref[pl.ds(r0, bsz), 0:pb] = nr.astype(BF16)
            x_ref[pl.ds(r0, bsz), pb:two_p] = ni.astype(BF16)
            return nr, ni

        xr, xi = lax.fori_loop(0, tt, step, (st_ref[j, 0], st_ref[j, 1]))
        st_ref[j, 0] = xr
        st_ref[j, 1] = xi
        y = _dot(x_ref[...], cm_ref[j]) + d_ref[:, cols] * uj
        ys_ref[:, cols] = _gelu_tanh(y)

    ys = ys_ref[...]
    z = _dot(ys.astype(BF16), gw_ref[...]) + gb_ref[...]
    o_ref[...] = ys * _sigmoid(z)


def _s5_glu(u_tm, a_blk, b_blk, c_blk, d_skip, glu_w, glu_b, *, bsz, tt):
    n, width = u_tm.shape
    nblk, _, two_p = b_blk.shape
    rows = tt * bsz
    full = lambda shape: pl.BlockSpec(shape, lambda i: (0,) * len(shape))
    return pl.pallas_call(
        functools.partial(_s5_kernel, bsz=bsz),
        grid=(n // rows,),
        in_specs=[pl.BlockSpec((rows, width), lambda i: (i, 0)),
                  full(a_blk.shape), full(b_blk.shape), full(c_blk.shape),
                  full((1, width)), full(glu_w.shape), full((1, width))],
        out_specs=pl.BlockSpec((rows, width), lambda i: (i, 0)),
        out_shape=jax.ShapeDtypeStruct((n, width), F32),
        scratch_shapes=[pltpu.VMEM((nblk, 2, bsz, two_p // 2), F32),
                        pltpu.VMEM((rows, two_p), F32),
                        pltpu.VMEM((rows, two_p), BF16),
                        pltpu.VMEM((rows, width), F32)],
        compiler_params=_cparams(("arbitrary",)),
        name="s5_glu",
    )(u_tm, a_blk, b_blk, c_blk, d_skip, glu_w, glu_b)


def _merge_kernel(x_ref, ya_ref, yb_ref, ga_ref, gb_ref, g1_ref, w_ref, o_ref):
    merged = _sigmoid(ga_ref[...]) * ya_ref[...] + _sigmoid(gb_ref[...]) * yb_ref[...]
    o_ref[...] = x_ref[...] + g1_ref[...] * _dot(merged.astype(BF16), w_ref[...])


def _merge_out(x, ya, yb_tm, proj, ga_col, gb_col, mod, g1_idx, w_out, *, tm):
    bsz, seq, d = x.shape
    tile = lambda: pl.BlockSpec((None, tm, d), lambda b, i: (b, i, 0))
    return pl.pallas_call(
        _merge_kernel,
        grid=(bsz, seq // tm),
        in_specs=[tile(), tile(),
                  pl.BlockSpec((tm, d), lambda b, i: (i, b)),
                  pl.BlockSpec((None, tm, d), lambda b, i: (b, i, ga_col)),
                  pl.BlockSpec((None, tm, d), lambda b, i: (b, i, gb_col)),
                  pl.BlockSpec((None, None, 1, d), lambda b, i: (b, g1_idx, 0, 0)),
                  pl.BlockSpec((d, d), lambda b, i: (0, 0))],
        out_specs=tile(),
        out_shape=jax.ShapeDtypeStruct((bsz, seq, d), F32),
        compiler_params=_cparams(("arbitrary", "arbitrary")),
        name="merge_out",
    )(x, ya, yb_tm, proj, proj, mod, w_out)


def _extract_topk(s_ref, tie_idx, k, on_select):
    big = jnp.int32(2 ** 30)
    for r in range(k):
        s = s_ref[...]
        m = jnp.max(s, axis=0, keepdims=True)
        first = jnp.min(jnp.where(s == m, tie_idx, big), axis=0, keepdims=True)
        onehot = tie_idx == first
        s_ref[...] = jnp.where(onehot, NEG_INF, s)
        on_select(r, m, first, onehot)


def _candidate_blocks(k):
    blocks = [(0, 1, 0, k, 0), (1, 1, 0, k // 2, 0), (k // 2, k // 2, 0, 1, 0)]
    r2 = 0
    while 3 * (r2 + 1) <= k:
        blocks.append((0, k // 2, r2, 1, 2))
        r2 += 1
    return blocks


def _peer_kernel(x_ref, ng_ref, sc_ref, sh_ref, g2_ref, fg_ref, wq_ref, keys_ref, u_ref, vt_ref,
                 o_ref,
                 h2_ref, s_ref, rank_ref, top_ref, cand_ref, n1_ref, e1_ref, e2_ref, r2_ref,
                 act_ref, a_ref, acc_ref, *, heads, final_norm):
    t, d = x_ref.shape
    nk = keys_ref.shape[1]
    k = PEER_TOPK
    j = pl.program_id(2)
    ch = u_ref.shape[0]
    per_chunk = ch // nk

    @pl.when(j == 0)
    def _():
        h2 = _norm_mod(x_ref[...], ng_ref[...], sc_ref[...], sh_ref[...]).astype(BF16)
        h2_ref[...] = h2
        acc_ref[...] = jnp.zeros_like(acc_ref)
        for h in range(heads):
            q_t = _dot_nt(wq_ref[h * 2 * nk:(h + 1) * 2 * nk, :], h2)
            for p in range(2):
                s_ref[2 * h + p] = _dot(keys_ref[2 * h + p], q_t[p * nk:(p + 1) * nk].astype(BF16))

        key_idx = lax.broadcasted_iota(jnp.int32, (nk, t), 0)
        top_row = lax.broadcasted_iota(jnp.int32, (k, t), 0)

        def rank_outer(hp, carry):
            cand_ref[0:nk, :] = s_ref[hp]
            rank_ref[hp] = jnp.full((nk, t), NOT_SELECTED, F32)

            def on_select(r, m, first, onehot):
                rank_ref[hp] = jnp.where(onehot, float(r), rank_ref[hp])
                top_ref[hp, r:r + 1, :] = m

            _extract_topk(cand_ref.at[0:nk], key_idx, k, on_select)
            return carry

        lax.fori_loop(0, 2 * heads, rank_outer, 0)

        blocks = _candidate_blocks(k)
        n_cand = sum(b[1] * b[3] for b in blocks)

        def head_body(h, carry):
            a = top_ref[2 * h]
            b = top_ref[2 * h + 1]
            off = 0
            flat_parts = []
            for (r1s, r1n, r2s, r2n, drop) in blocks:
                n = r1n * r2n
                if r1n == 1:
                    blk = a[r1s:r1s + 1] + b[r2s:r2s + r2n]
                    flat = r1s * k + r2s + lax.broadcasted_iota(jnp.int32, (n, t), 0)
                else:
                    blk = a[r1s:r1s + r1n] + b[r2s:r2s + 1]
                    r1 = r1s + lax.broadcasted_iota(jnp.int32, (n, t), 0)
                    flat = r1 * k + r2s
                    if drop:
                        blk = jnp.where(r1 < r1s + drop, NEG_INF, blk)
                cand_ref[off:off + n, :] = blk
                flat_parts.append(flat)
                off += n
            flat_idx = jnp.concatenate(flat_parts, axis=0)

            best = a[0:1] + b[0:1]
            state = {"n": jnp.zeros((k, t), F32), "z": jnp.zeros((1, t), F32)}

            def on_select(r, m, first, onehot):
                state["n"] = state["n"] + jnp.where(top_row == (first >> (k.bit_length() - 1)), 1.0, 0.0)
                state["z"] = state["z"] + jnp.exp(m - best)

            _extract_topk(cand_ref.at[0:n_cand], flat_idx, k, on_select)
            n_sel, z = state["n"], state["z"]

            r1 = rank_ref[2 * h]
            n1 = jnp.zeros((nk, t), F32)
            for r in range(k):
                n1 = jnp.where(r1 == float(r), n_sel[r:r + 1], n1)
            n1_ref[h] = n1
            e1_ref[h] = jnp.exp(s_ref[2 * h] - a[0:1]) / z
            e2_ref[h] = jnp.exp(s_ref[2 * h + 1] - b[0:1]).astype(BF16)
            r2_ref[h] = rank_ref[2 * h + 1].astype(BF16)
            return carry

        lax.fori_loop(0, heads, head_body, 0)

    act_ref[...] = _dot_nt(u_ref[...], h2_ref[...])

    def key_body(ii, carry):
        i1 = j * per_chunk + ii
        r0 = pl.multiple_of(ii * nk, nk)
        w = jnp.zeros((nk, t), BF16)
        for h in range(heads):
            n_row = n1_ref[h, pl.ds(i1, 1), :].astype(BF16)
            e_row = e1_ref[h, pl.ds(i1, 1), :].astype(BF16)
            w = w + jnp.where(r2_ref[h] < n_row, e2_ref[h], jnp.zeros((), BF16)) * e_row
        g = _gelu_tanh(act_ref[pl.ds(r0, nk), :])
        a_ref[pl.ds(r0, nk), :] = g.astype(BF16) * w
        return carry

    lax.fori_loop(0, per_chunk, key_body, 0)
    acc_ref[...] += _dot(vt_ref[...], a_ref[...])

    @pl.when(j == pl.num_programs(2) - 1)
    def _():
        y = x_ref[...] + g2_ref[...] * acc_ref[...].T
        if final_norm:
            y = y * lax.rsqrt(jnp.mean(y * y, axis=-1, keepdims=True) + EPS) * fg_ref[...]
        o_ref[...] = y


def _peer(x, norm_g, mod, sc_idx, sh_idx, g_idx, final_g, wq_t, keys, u_tab, v_tab_t,
          *, t, ch, final_norm):
    bsz, seq, d = x.shape
    heads = keys.shape[0] // 2
    nk = keys.shape[1]
    k = PEER_TOPK
    n_exp = u_tab.shape[0]
    n_cand = sum(b[1] * b[3] for b in _candidate_blocks(k))
    cand_rows = max(nk, -(-n_cand // 8) * 8)
    modrow = lambda idx: pl.BlockSpec((None, None, 1, d), lambda b, i, j: (b, idx, 0, 0))
    const = lambda shape: pl.BlockSpec(shape, lambda b, i, j: (0,) * len(shape))
    return pl.pallas_call(
        functools.partial(_peer_kernel, heads=heads, final_norm=final_norm),
        grid=(bsz, seq // t, n_exp // ch),
        in_specs=[pl.BlockSpec((None, t, d), lambda b, i, j: (b, i, 0)),
                  const((1, d)), modrow(sc_idx), modrow(sh_idx), modrow(g_idx), const((1, d)),
                  const(wq_t.shape), const(keys.shape),
                  pl.BlockSpec((ch, d), lambda b, i, j: (j, 0)),
                  pl.BlockSpec((d, ch), lambda b, i, j: (0, j))],
        out_specs=pl.BlockSpec((None, t, d), lambda b, i, j: (b, i, 0)),
        out_shape=jax.ShapeDtypeStruct((bsz, seq, d), F32),
        scratch_shapes=[pltpu.VMEM((t, d), BF16),
                        pltpu.VMEM((2 * heads, nk, t), F32),
                        pltpu.VMEM((2 * heads, nk, t), F32),
                        pltpu.VMEM((2 * heads, k, t), F32),
                        pltpu.VMEM((cand_rows, t), F32),
                        pltpu.VMEM((heads, nk, t), F32),
                        pltpu.VMEM((heads, nk, t), F32),
                        pltpu.VMEM((heads, nk, t), BF16),
                        pltpu.VMEM((heads, nk, t), BF16),
                        pltpu.VMEM((ch, t), F32),
                        pltpu.VMEM((ch, t), BF16),
                        pltpu.VMEM((d, t), F32)],
        compiler_params=_cparams(("arbitrary", "arbitrary", "arbitrary")),
        name="peer",
    )(x, norm_g, mod, mod, mod, final_g, wq_t, keys, u_tab, v_tab_t)


def _s5_block_params(ab_re, ab_im, bb_re, bb_im, c_re, c_im):
    g, c, p = bb_re.shape
    gb = V7X_LANES // c
    nb = g // gb
    eye = jnp.eye(gb, dtype=F32)

    def in_mat(bb):
        x = bb.reshape(nb, gb, c, p)
        return jnp.einsum("ngcp,gh->ngchp", x, eye).reshape(nb, gb * c, gb * p)

    def out_mat(cc):
        x = cc.reshape(nb, gb, c, p)
        return jnp.einsum("ngcp,gh->ngphc", x, eye).reshape(nb, gb * p, gb * c)

    a_blk = jnp.stack([ab_re.reshape(nb, gb * p), ab_im.reshape(nb, gb * p)], axis=1)
    b_blk = jnp.concatenate([in_mat(bb_re), in_mat(bb_im)], axis=-1).astype(BF16)
    c_blk = jnp.concatenate([out_mat(c_re), -out_mat(c_im)], axis=1).astype(BF16)
    return a_blk, b_blk, c_blk


def kernel(x, c, ada_w, ada_b, norm1_g, norm2_g, w_in, hg_lb_logits, hg_norm_g, s5_a_re, s5_a_im, s5_log_dt, s5_b_re, s5_b_im, s5_c_re, s5_c_im, s5_d, glu_w, glu_b, w_out, peer_wq, peer_subkeys, peer_u, peer_v, final_g):
    bsz, seq, d = x.shape
    depth = ada_w.shape[0]
    hg_w = hg_lb_logits.shape[1]
    s5_w = s5_d.shape[1]
    groups, states = s5_a_re.shape[1], s5_a_re.shape[2]
    n_heads = peer_subkeys.shape[1]
    nk, half = peer_subkeys.shape[3], peer_subkeys.shape[4]

    tm = min(1024, seq)
    mod_all = _adaln_mod(c, ada_w, ada_b).reshape(depth, bsz, N_MOD, 1, d)
    lb_all = _lower_bounds(hg_lb_logits)

    for l in range(depth):
        mod = mod_all[l]
        w_l = w_in[l].astype(BF16)
        w_bm = jnp.concatenate([w_l[:, :4 * hg_w], w_l[:, 4 * hg_w + s5_w:]], axis=1)
        w_s5 = w_l[:, 4 * hg_w:4 * hg_w + s5_w]
        g1n = norm1_g[l].reshape(1, d)
        proj = _norm_proj(x, g1n, mod, 1, 0, w_bm, tm=tm, tn=1024, time_major=False)
        u_tm = _norm_proj(x, g1n, mod, 1, 0, w_s5, tm=tm, tn=1024, time_major=True)
        u_tm = u_tm.reshape(seq * bsz, s5_w)

        ya = _hgrn2(proj, lb_all[l], hg_norm_g[l].reshape(1, -1), tt=min(256, seq), heads=HG_HEADS)

        ab_re, ab_im, bb_re, bb_im = _s5_discretise(
            s5_a_re[l], s5_a_im[l], s5_log_dt[l],
            s5_b_re[l].transpose(0, 2, 1), s5_b_im[l].transpose(0, 2, 1))
        a_blk, b_blk, c_blk = _s5_block_params(
            ab_re.reshape(groups, states), ab_im.reshape(groups, states), bb_re, bb_im,
            s5_c_re[l], s5_c_im[l])
        yb_tm = _s5_glu(u_tm, a_blk, b_blk, c_blk, s5_d[l].reshape(1, -1),
                        glu_w[l].astype(BF16), glu_b[l].reshape(1, -1), bsz=bsz, tt=min(32, seq))
        yb_tm = yb_tm.reshape(seq, bsz * s5_w)

        x = _merge_out(x, ya, yb_tm, proj, 4, 5, mod, 2, w_out[l].astype(BF16), tm=tm)

        wq_t = peer_wq[l].T.astype(BF16)
        keys = peer_subkeys[l].reshape(2 * n_heads, nk, half).astype(BF16)
        x = _peer(x, norm2_g[l].reshape(1, d), mod, 4, 3, 5, final_g.reshape(1, d), wq_t, keys,
                  peer_u[l].astype(BF16), peer_v[l].T.astype(BF16),
                  t=min(512, seq), ch=1024, final_norm=(l == depth - 1))
    return x
```
